```python
import jax, jax.numpy as jnp
from jax import lax
import numpy as np

D_MODEL = 1024
BATCH = 2
SEQ = 8192
DEPTH = 2
DEC_BATCH = 32
DEC_SEQ = 1
PAST_LEN = 16384
PAGE_SIZE = 128

N_A = DEPTH // 2
N_B = DEPTH - N_A
N_HEADS = 16
HEAD_DIM = D_MODEL // N_HEADS
CONV_WIDTH = 31
CONV_CTX = CONV_WIDTH - 1
D_FF = -(-8 * D_MODEL // (3 * 256)) * 256
Q_BLOCK = 128
EPS = 1e-6
SCALE = HEAD_DIM ** -0.5

kernel_name = 'yoco_conformer_conv_fox_decode_step'


def rmsnorm(x, g):
    xf = x.astype(jnp.float32)
    y = xf * lax.rsqrt(jnp.mean(xf * xf, axis=-1, keepdims=True) + EPS)
    return (y * g.astype(jnp.float32)).astype(x.dtype)


def layernorm(x, g, b):
    xf = x.astype(jnp.float32)
    mu = jnp.mean(xf, axis=-1, keepdims=True)
    var = jnp.mean(jnp.square(xf - mu), axis=-1, keepdims=True)
    y = (xf - mu) * lax.rsqrt(var + EPS)
    return (y * g.astype(jnp.float32) + b.astype(jnp.float32)).astype(x.dtype)


def swiglu(x, wg, wu, wd):
    return (jax.nn.silu(x @ wg) * (x @ wu)) @ wd


def conv_module(xn, ctx, w_pw1, w_dw, b_dw, ln_g, ln_b, w_pw2):
    a, g = jnp.split(xn @ w_pw1, 2, axis=-1)
    u = a * jax.nn.sigmoid(g)
    u_ext = jnp.concatenate([ctx.astype(u.dtype), u], axis=1)
    c = lax.conv_general_dilated(
        u_ext, w_dw[:, None, :].astype(u.dtype), window_strides=(1,), padding='VALID',
        dimension_numbers=('NWC', 'WIO', 'NWC'), feature_group_count=D_MODEL) + b_dw
    h = jax.nn.silu(layernorm(c, ln_g, ln_b))
    return h @ w_pw2, u_ext[:, -CONV_CTX:]


def shared_kv(h, norm_kv, w_k, w_v, w_f, b_f):
    n, t, _ = h.shape
    s = rmsnorm(h, norm_kv)
    k = (s @ w_k).reshape(n, t, N_HEADS, HEAD_DIM)
    v = (s @ w_v).reshape(n, t, N_HEADS, HEAD_DIM)
    logf = jax.nn.log_sigmoid((s @ w_f + b_f).astype(jnp.float32))
    return k, v, logf


def fox_prompt(q, k, v, logf):
    n, t, _, _ = q.shape
    F = jnp.cumsum(logf, axis=1)
    Fk = F.transpose(0, 2, 1)[:, :, None, :]
    kpos = jnp.arange(t)

    def block(i):
        start = i * Q_BLOCK
        qb = lax.dynamic_slice_in_dim(q, start, Q_BLOCK, axis=1)
        Fq = lax.dynamic_slice_in_dim(F, start, Q_BLOCK, axis=1).transpose(0, 2, 1)[..., None]
        s = jnp.einsum('bqhd,bkhd->bhqk', qb, k).astype(jnp.float32) * SCALE + (Fq - Fk)
        qpos = start + jnp.arange(Q_BLOCK)
        s = jnp.where(qpos[:, None] >= kpos[None, :], s, -jnp.inf)
        p = jax.nn.softmax(s, axis=-1)
        return jnp.einsum('bhqk,bkhd->bqhd', p.astype(v.dtype), v)

    o = lax.map(block, jnp.arange(t // Q_BLOCK))
    return o.transpose(1, 0, 2, 3, 4).reshape(n, t, N_HEADS * HEAD_DIM)


def fox_sample(q, k_past, v_past, logf_past, k_new, v_new, logf_new):
    n, tn, _, _ = q.shape
    p_len = k_past.shape[1]
    F = jnp.cumsum(jnp.concatenate([logf_past.astype(jnp.float32), logf_new.astype(jnp.float32)], axis=1), axis=1)
    Fq = F[:, p_len:].transpose(0, 2, 1)[..., None]
    Fk = F.transpose(0, 2, 1)[:, :, None, :]
    s = jnp.concatenate([jnp.einsum('bqhd,bkhd->bhqk', q, k_past),
                         jnp.einsum('bqhd,bkhd->bhqk', q, k_new)], axis=-1)
    s = s.astype(jnp.float32) * SCALE + (Fq - Fk)
    kpos = jnp.arange(p_len + tn)
    qpos = p_len + jnp.arange(tn)
    s = jnp.where(qpos[:, None] >= kpos[None, :], s, -jnp.inf)
    p = jax.nn.softmax(s, axis=-1).astype(v_past.dtype)
    o = (jnp.einsum('bhqk,bkhd->bqhd', p[..., :p_len], v_past)
         + jnp.einsum('bhqk,bkhd->bqhd', p[..., p_len:], v_new))
    return o.reshape(n, tn, N_HEADS * HEAD_DIM)


def setup_inputs(seed: int = 0) -> dict:
    key = jax.random.key(seed)
    ks = jax.random.split(key, 32)
    f32 = jnp.float32
    n_pages = PAST_LEN // PAGE_SIZE
    n_used = DEC_BATCH * n_pages
    n_pool = n_used + n_used // 4
    nrm = lambda k, shape, scale: jax.random.normal(k, shape, f32) * scale
    page_table = jax.random.permutation(ks[0], n_pool)[:n_used].reshape(DEC_BATCH, n_pages).astype(jnp.int32)
    return {
        'x_prompt': nrm(ks[1], (BATCH, SEQ, D_MODEL), 1.0),
        'x_sample': nrm(ks[2], (DEC_BATCH, DEC_SEQ, D_MODEL), 1.0),
        'cache_k': nrm(ks[3], (n_pool, PAGE_SIZE, N_HEADS, HEAD_DIM), 1.0),
        'cache_v': nrm(ks[4], (n_pool, PAGE_SIZE, N_HEADS, HEAD_DIM), 1.0),
        'cache_logf': jax.nn.log_sigmoid(2.0 + nrm(ks[5], (n_pool, PAGE_SIZE, N_HEADS), 1.0)),
        'state_conv': nrm(ks[6], (N_A, DEC_BATCH, CONV_CTX, D_MODEL), 0.5),
        'page_table': page_table,
        'norm_mix': 1.0 + nrm(ks[7], (DEPTH, D_MODEL), 0.02),
        'norm_ffn': 1.0 + nrm(ks[8], (DEPTH, D_MODEL), 0.02),
        'w_pw1': nrm(ks[9], (N_A, D_MODEL, 2 * D_MODEL), D_MODEL ** -0.5),
        'w_dw': nrm(ks[10], (N_A, CONV_WIDTH, D_MODEL), CONV_WIDTH ** -0.5),
        'b_dw': nrm(ks[11], (N_A, D_MODEL), 0.02),
        'conv_ln_g': 1.0 + nrm(ks[12], (N_A, D_MODEL), 0.02),
        'conv_ln_b': nrm(ks[13], (N_A, D_MODEL), 0.02),
        'w_pw2': nrm(ks[14], (N_A, D_MODEL, D_MODEL), D_MODEL ** -0.5),
        'norm_kv': 1.0 + nrm(ks[15], (D_MODEL,), 0.02),
        'w_k': nrm(ks[16], (D_MODEL, N_HEADS * HEAD_DIM), D_MODEL ** -0.5),
        'w_v': nrm(ks[17], (D_MODEL, N_HEADS * HEAD_DIM), D_MODEL ** -0.5),
        'w_f': nrm(ks[18], (D_MODEL, N_HEADS), D_MODEL ** -0.5),
        'b_f': 2.0 + nrm(ks[19], (N_HEADS,), 0.5),
        'w_q': nrm(ks[20], (N_B, D_MODEL, N_HEADS * HEAD_DIM), D_MODEL ** -0.5),
        'w_o': nrm(ks[21], (N_B, N_HEADS * HEAD_DIM, D_MODEL), (N_HEADS * HEAD_DIM) ** -0.5),
        'w_gate': nrm(ks[22], (DEPTH, D_MODEL, D_FF), D_MODEL ** -0.5),
        'w_up': nrm(ks[23], (DEPTH, D_MODEL, D_FF), D_MODEL ** -0.5),
        'w_down': nrm(ks[24], (DEPTH, D_FF, D_MODEL), D_FF ** -0.5),
        'norm_final': 1.0 + nrm(ks[25], (D_MODEL,), 0.02),
    }


def reference(x_prompt, x_sample, cache_k, cache_v, cache_logf, state_conv, page_table,
              norm_mix, norm_ffn, w_pw1, w_dw, b_dw, conv_ln_g, conv_ln_b, w_pw2,
              norm_kv, w_k, w_v, w_f, b_f, w_q, w_o, w_gate, w_up, w_down, norm_final):
    hp, hs = x_prompt, x_sample
    n_s = x_sample.shape[0]
    zero_ctx = jnp.zeros((x_prompt.shape[0], CONV_CTX, D_MODEL), x_prompt.dtype)
    conv_p, conv_s = [], []
    kp = vp = lfp = ks_ = vs_ = lfs_ = None
    for layer in range(DEPTH):
        if layer < N_A:
            a = layer
            cw = (w_pw1[a], w_dw[a], b_dw[a], conv_ln_g[a], conv_ln_b[a], w_pw2[a])
            op, cp = conv_module(rmsnorm(hp, norm_mix[layer]), zero_ctx, *cw)
            os_, cs = conv_module(rmsnorm(hs, norm_mix[layer]), state_conv[a], *cw)
            hp, hs = hp + op, hs + os_
            conv_p.append(cp)
            conv_s.append(cs)
        else:
            if layer == N_A:
                kp, vp, lfp = shared_kv(hp, norm_kv, w_k, w_v, w_f, b_f)
                ks_, vs_, lfs_ = shared_kv(hs, norm_kv, w_k, w_v, w_f, b_f)
                k_past = cache_k[page_table].reshape(n_s, -1, N_HEADS, HEAD_DIM)
                v_past = cache_v[page_table].reshape(n_s, -1, N_HEADS, HEAD_DIM)
                lf_past = cache_logf[page_table].reshape(n_s, -1, N_HEADS)
            b = layer - N_A
            qp = (rmsnorm(hp, norm_mix[layer]) @ w_q[b]).reshape(hp.shape[0], hp.shape[1], N_HEADS, HEAD_DIM)
            qs = (rmsnorm(hs, norm_mix[layer]) @ w_q[b]).reshape(hs.shape[0], hs.shape[1], N_HEADS, HEAD_DIM)
            hp = hp + fox_prompt(qp, kp, vp, lfp) @ w_o[b]
            hs = hs + fox_sample(qs, k_past, v_past, lf_past, ks_, vs_, lfs_) @ w_o[b]
        hp = hp + swiglu(rmsnorm(hp, norm_ffn[layer]), w_gate[layer], w_up[layer], w_down[layer])
        hs = hs + swiglu(rmsnorm(hs, norm_ffn[layer]), w_gate[layer], w_up[layer], w_down[layer])
    y_prompt = rmsnorm(hp, norm_final)
    y_sample = rmsnorm(hs, norm_final)
    conv_state_prompt = jnp.stack(conv_p)
    conv_state_sample = jnp.stack(conv_s)
    return (y_prompt, y_sample, conv_state_prompt, conv_state_sample, kp, vp, lfp, ks_, vs_, lfs_)
```

```python
import functools

import jax
import jax.numpy as jnp
from jax import lax
from jax.experimental import pallas as pl
from jax.experimental.pallas import tpu as pltpu

EPS = 1e-6
NEG_BIG = -1e30
LANES = 128
SUBLANES = 8
CTX_PAD = 32
CUMSUM_CHUNK = 256
VMEM_LIMIT = 48 * 1024 * 1024

_bf16 = jnp.bfloat16
_f32 = jnp.float32


def _const_spec(shape):
    return pl.BlockSpec(shape, lambda *_: (0,) * len(shape), pipeline_mode=pl.Buffered(1))


def _params(*semantics):
    return pltpu.CompilerParams(dimension_semantics=semantics, vmem_limit_bytes=VMEM_LIMIT)


def _rms_hat(x):
    return x * lax.rsqrt(jnp.mean(x * x, axis=-1, keepdims=True) + EPS)


def _sigmoid(x):
    return 1.0 / (1.0 + jnp.exp(-x))


def _dot(a, b):
    return jnp.dot(a, b, preferred_element_type=_f32)


def _dot_nt(a, b):
    return lax.dot_general(a, b, (((1,), (1,)), ((), ())), preferred_element_type=_f32)


def _split3(x):
    hi = x.astype(_bf16)
    r1 = x - hi.astype(_f32)
    mid = r1.astype(_bf16)
    lo = (r1 - mid.astype(_f32)).astype(_bf16)
    return hi, mid, lo


def _glu(x, g, w_pw1):
    d = x.shape[-1]
    xn = (_rms_hat(x) * g).astype(_bf16)
    ag = _dot(xn, w_pw1)
    return ag[:, :d] * _sigmoid(ag[:, d:])


def _conv_tail(c, x, ln_g, ln_b, w_pw2):
    mu = jnp.mean(c, axis=-1, keepdims=True)
    cc = c - mu
    var = jnp.mean(cc * cc, axis=-1, keepdims=True)
    y = cc * lax.rsqrt(var + EPS) * ln_g + ln_b
    h = (y * _sigmoid(y)).astype(_bf16)
    return x + _dot(h, w_pw2)


def _conv_prompt_kernel(x_ref, g_ref, wpw1_ref, wdw_ref, bdw_ref, lng_ref, lnb_ref, wpw2_ref,
                        h_ref, state_ref, u_scr, c_scr, *, tm, width, row_chunk, lane_chunk):
    t = pl.program_id(1)
    ctx = width - 1
    d = x_ref.shape[-1]

    @pl.when(t == 0)
    def _():
        u_scr[0:CTX_PAD, :] = jnp.zeros((CTX_PAD, d), _f32)
        u_scr[CTX_PAD + tm:CTX_PAD + tm + SUBLANES, :] = jnp.zeros((SUBLANES, d), _f32)

    x = x_ref[...]
    u_scr[CTX_PAD:CTX_PAD + tm, :] = _glu(x, g_ref[...], wpw1_ref[...])

    base = CTX_PAD - ctx
    halo = CTX_PAD + SUBLANES

    def conv_rows(i, _):
        r0 = pl.multiple_of(i * row_chunk, row_chunk)
        for l0 in range(0, d, lane_chunk):
            lanes = slice(l0, l0 + lane_chunk)
            blk = u_scr[pl.ds(r0, row_chunk + halo), lanes]
            acc = jnp.broadcast_to(bdw_ref[:, lanes], (row_chunk, lane_chunk))
            for shift in range(SUBLANES):
                part = None
                for off in range(shift, base + width, SUBLANES):
                    if off < base:
                        continue
                    term = blk[off - shift:off - shift + row_chunk + SUBLANES] * wdw_ref[off - base:off - base + 1, lanes]
                    part = term if part is None else part + term
                acc = acc + part[shift:shift + row_chunk]
            c_scr[pl.ds(r0, row_chunk), lanes] = acc
        return 0

    lax.fori_loop(0, tm // row_chunk, conv_rows, 0)

    h_ref[...] = _conv_tail(c_scr[...], x, lng_ref[...], lnb_ref[...], wpw2_ref[...])

    @pl.when(t == pl.num_programs(1) - 1)
    def _():
        state_ref[...] = u_scr[CTX_PAD + tm - ctx:CTX_PAD + tm, :]

    u_scr[0:CTX_PAD, :] = u_scr[tm:tm + CTX_PAD, :]


def _conv_prompt(x, g, w_pw1, w_dw, b_dw, ln_g, ln_b, w_pw2, *, tm=512):
    b, t, d = x.shape
    width = w_dw.shape[0]
    kern = functools.partial(_conv_prompt_kernel, tm=tm, width=width, row_chunk=64, lane_chunk=256)
    return pl.pallas_call(
        kern,
        grid=(b, t // tm),
        in_specs=[
            pl.BlockSpec((None, tm, d), lambda i, j: (i, j, 0)),
            _const_spec((1, d)), _const_spec(w_pw1.shape), _const_spec(w_dw.shape),
            _const_spec((1, d)), _const_spec((1, d)), _const_spec((1, d)), _const_spec(w_pw2.shape),
        ],
        out_specs=[
            pl.BlockSpec((None, tm, d), lambda i, j: (i, j, 0)),
            pl.BlockSpec((None, width - 1, d), lambda i, j: (i, 0, 0)),
        ],
        out_shape=[
            jax.ShapeDtypeStruct((b, t, d), _f32),
            jax.ShapeDtypeStruct((b, width - 1, d), _f32),
        ],
        scratch_shapes=[pltpu.VMEM((CTX_PAD + tm + SUBLANES, d), _f32), pltpu.VMEM((tm, d), _f32)],
        compiler_params=_params("arbitrary", "arbitrary"),
        name="conv_mixer_prompt",
    )(x, g, w_pw1, w_dw, b_dw, ln_g, ln_b, w_pw2)


def _conv_sample_kernel(x_ref, st_ref, g_ref, wpw1_ref, wdw_ref, bdw_ref, lng_ref, lnb_ref, wpw2_ref,
                        h_ref, u_ref, *, width):
    x = x_ref[...]
    u = _glu(x, g_ref[...], wpw1_ref[...])
    u_ref[...] = u
    c = bdw_ref[...] + u * wdw_ref[width - 1:width, :]
    for w in range(width - 1):
        c = c + st_ref[w] * wdw_ref[w:w + 1, :]
    h_ref[...] = _conv_tail(c, x, lng_ref[...], lnb_ref[...], wpw2_ref[...])


def _conv_sample(x, state_t, g, w_pw1, w_dw, b_dw, ln_g, ln_b, w_pw2):
    n, d = x.shape
    width = w_dw.shape[0]
    kern = functools.partial(_conv_sample_kernel, width=width)
    return pl.pallas_call(
        kern,
        grid=(1,),
        in_specs=[_const_spec(x.shape), _const_spec(state_t.shape), _const_spec((1, d)),
                  _const_spec(w_pw1.shape), _const_spec(w_dw.shape), _const_spec((1, d)),
                  _const_spec((1, d)), _const_spec((1, d)), _const_spec(w_pw2.shape)],
        out_specs=[_const_spec((n, d)), _const_spec((n, d))],
        out_shape=[jax.ShapeDtypeStruct((n, d), _f32), jax.ShapeDtypeStruct((n, d), _f32)],
        compiler_params=_params("arbitrary"),
        name="conv_mixer_sample",
    )(x, state_t, g, w_pw1, w_dw, b_dw, ln_g, ln_b, w_pw2)


def _ffn_kernel(*refs, ff_chunk, attn_proj, final_norm):
    it = iter(refs)
    h_ref = next(it)
    if attn_proj:
        o_ref, wo_ref = next(it), next(it)
    g_ref, wg_ref, wu_ref, wd_ref = next(it), next(it), next(it), next(it)
    if final_norm:
        gf_ref = next(it)
    out_ref, mid_scr = next(it), next(it)

    h = h_ref[...]
    if attn_proj:
        h = h + _dot(o_ref[...], wo_ref[...])
    xn = (_rms_hat(h) * g_ref[...]).astype(_bf16)
    d_ff = wg_ref.shape[1]
    for c0 in range(0, d_ff, ff_chunk):
        gate = _dot(xn, wg_ref[:, c0:c0 + ff_chunk])
        up = _dot(xn, wu_ref[:, c0:c0 + ff_chunk])
        mid_scr[:, c0:c0 + ff_chunk] = (gate * _sigmoid(gate) * up).astype(_bf16)
    h = h + _dot(mid_scr[...], wd_ref[...])
    if final_norm:
        h = _rms_hat(h) * gf_ref[...]
    out_ref[...] = h


def _ffn(h, g, w_gate, w_up, w_down, *, tm, o=None, w_o=None, g_final=None):
    m, d = h.shape
    d_ff = w_gate.shape[1]
    attn_proj = o is not None
    final_norm = g_final is not None
    row = pl.BlockSpec((tm, d), lambda i: (i, 0))
    args, specs = [h], [row]
    if attn_proj:
        args += [o, w_o]
        specs += [row, _const_spec(w_o.shape)]
    args += [g, w_gate, w_up, w_down]
    specs += [_const_spec((1, d)), _const_spec(w_gate.shape), _const_spec(w_up.shape),
              _const_spec(w_down.shape)]
    if final_norm:
        args.append(g_final)
        specs.append(_const_spec((1, d)))
    kern = functools.partial(_ffn_kernel, ff_chunk=256, attn_proj=attn_proj, final_norm=final_norm)
    return pl.pallas_call(
        kern,
        grid=(m // tm,),
        in_specs=specs,
        out_specs=row,
        out_shape=jax.ShapeDtypeStruct((m, d), _f32),
        scratch_shapes=[pltpu.VMEM((tm, d_ff), _bf16)],
        compiler_params=_params("arbitrary"),
        name="ffn_attnproj" if attn_proj else "ffn",
    )(*args)


def _kvq_kernel(h_ref, gkv_ref, gq_ref, wk_ref, wv_ref, wf_ref, bf_ref, wq_ref,
                k_ref, v_ref, lf_ref, kb_ref, vb_ref, qb_ref, *, scale):
    xh = _rms_hat(h_ref[...])
    s = (xh * gkv_ref[...]).astype(_bf16)
    k = _dot(s, wk_ref[...])
    v = _dot(s, wv_ref[...])
    n_heads = lf_ref.shape[-1]
    f = _dot(s, wf_ref[...])[:, :n_heads] + bf_ref[...]
    lf_ref[...] = -(jnp.maximum(-f, 0.0) + jnp.log(1.0 + jnp.exp(-jnp.abs(f))))
    k_ref[...] = k
    v_ref[...] = v
    kb_ref[...] = k.astype(_bf16)
    vb_ref[...] = v.astype(_bf16)
    qn = (xh * gq_ref[...]).astype(_bf16)
    qb_ref[...] = (_dot(qn, wq_ref[...]) * scale).astype(_bf16)


def _kvq(h, g_kv, g_q, w_k, w_v, w_f_pad, b_f, w_q, *, tm, scale):
    m, d = h.shape
    n_heads = b_f.shape[-1]
    row = pl.BlockSpec((tm, d), lambda i: (i, 0))
    kern = functools.partial(_kvq_kernel, scale=scale)
    return pl.pallas_call(
        kern,
        grid=(m // tm,),
        in_specs=[row, _const_spec((1, d)), _const_spec((1, d)), _const_spec(w_k.shape),
                  _const_spec(w_v.shape), _const_spec(w_f_pad.shape), _const_spec((1, n_heads)),
                  _const_spec(w_q.shape)],
        out_specs=[row, row, pl.BlockSpec((tm, n_heads), lambda i: (i, 0)), row, row, row],
        out_shape=[jax.ShapeDtypeStruct((m, d), _f32), jax.ShapeDtypeStruct((m, d), _f32),
                   jax.ShapeDtypeStruct((m, n_heads), _f32), jax.ShapeDtypeStruct((m, d), _bf16),
                   jax.ShapeDtypeStruct((m, d), _bf16), jax.ShapeDtypeStruct((m, d), _bf16)],
        compiler_params=_params("arbitrary"),
        name="kvq_proj",
    )(h, g_kv, g_q, w_k, w_v, w_f_pad, b_f, w_q)


def _cumsum_kernel(x_ref, f_ref):
    n_rows, t = x_ref.shape
    r = lax.broadcasted_iota(jnp.int32, (CUMSUM_CHUNK, CUMSUM_CHUNK), 0)
    c = lax.broadcasted_iota(jnp.int32, (CUMSUM_CHUNK, CUMSUM_CHUNK), 1)
    upper = jnp.where(r <= c, 1.0, 0.0).astype(_bf16)
    carry = jnp.zeros((n_rows, 1), _f32)
    for c0 in range(0, t, CUMSUM_CHUNK):
        hi, mid, lo = _split3(x_ref[:, c0:c0 + CUMSUM_CHUNK])
        loc = _dot(hi, upper) + _dot(mid, upper) + _dot(lo, upper) + carry
        f_ref[:, c0:c0 + CUMSUM_CHUNK] = loc
        carry = loc[:, CUMSUM_CHUNK - 1:CUMSUM_CHUNK]


def _cumsum_time(x):
    b, rows, t = x.shape
    spec = pl.BlockSpec((None, rows, t), lambda i: (i, 0, 0))
    return pl.pallas_call(
        _cumsum_kernel, grid=(b,), in_specs=[spec], out_specs=spec,
        out_shape=jax.ShapeDtypeStruct(x.shape, _f32),
        compiler_params=_params("arbitrary"), name="logf_cumsum",
    )(x)


def _attn_prompt_kernel(q_ref, k_ref, v_ref, f_ref, o_ref, *, tq, head_dim):
    qi = pl.program_id(2)
    q = q_ref[...]
    lane = lax.broadcasted_iota(jnp.int32, q.shape, 1)
    zero = jnp.zeros_like(q)
    qs = (jnp.where(lane < head_dim, q, zero), jnp.where(lane >= head_dim, q, zero))

    def step(start, carry, diagonal):
        k = k_ref[pl.ds(start, tq), :]
        v = v_ref[pl.ds(start, tq), :]
        out = []
        for hh in range(2):
            m, l, acc = carry[hh]
            s = _dot_nt(qs[hh], k) - f_ref[hh:hh + 1, pl.ds(start, tq)]
            if diagonal:
                row = lax.broadcasted_iota(jnp.int32, s.shape, 0)
                col = lax.broadcasted_iota(jnp.int32, s.shape, 1)
                s = jnp.where(row >= col, s, NEG_BIG)
            m_new = jnp.maximum(m, jnp.max(s, axis=-1, keepdims=True))
            alpha = jnp.exp(m - m_new)
            p = jnp.exp(s - m_new)
            l = alpha * l + jnp.sum(p, axis=-1, keepdims=True)
            acc = alpha * acc + _dot(p.astype(_bf16), v)
            out.append((m_new, l, acc))
        return tuple(out)

    init = tuple((jnp.full((tq, 1), NEG_BIG, _f32), jnp.zeros((tq, 1), _f32),
                  jnp.zeros((tq, 2 * head_dim), _f32)) for _ in range(2))
    carry = lax.fori_loop(
        0, qi, lambda i, c: step(pl.multiple_of(i * tq, tq), c, False), init)
    (_, l0, a0), (_, l1, a1) = step(pl.multiple_of(qi * tq, tq), carry, True)
    o_ref[...] = jnp.where(lane < head_dim, a0 / l0, a1 / l1).astype(o_ref.dtype)


def _attn_prompt(q, k, v, f, *, head_dim, tq=512):
    b, t, d = q.shape
    pair = 2 * head_dim
    kern = functools.partial(_attn_prompt_kernel, tq=tq, head_dim=head_dim)
    q_spec = pl.BlockSpec((None, tq, pair), lambda i, j, n: (i, n, j))
    kv_spec = pl.BlockSpec((None, t, pair), lambda i, j, n: (i, 0, j))
    return pl.pallas_call(
        kern,
        grid=(b, d // pair, t // tq),
        in_specs=[q_spec, kv_spec, kv_spec,
                  pl.BlockSpec((None, None, 2, t), lambda i, j, n: (i, j, 0, 0))],
        out_specs=q_spec,
        out_shape=jax.ShapeDtypeStruct((b, t, d), _bf16),
        compiler_params=_params("arbitrary", "arbitrary", "arbitrary"),
        name="fox_attn_prompt",
    )(q, k, v, f)


def _decode_bias_kernel(pt_ref, lfn_ref, *refs, pages):
    lf_refs, out_ref, carry_scr = refs[:pages], refs[pages], refs[pages + 1]
    page = lf_refs[0].shape[0]
    chunk_pages = CUMSUM_CHUNK // page

    @pl.when(pl.program_id(1) == 0)
    def _():
        carry_scr[...] = lfn_ref[...]

    r = lax.broadcasted_iota(jnp.int32, (CUMSUM_CHUNK, CUMSUM_CHUNK), 0)
    c = lax.broadcasted_iota(jnp.int32, (CUMSUM_CHUNK, CUMSUM_CHUNK), 1)
    later = jnp.where(c > r, 1.0, 0.0).astype(_bf16)
    carry = carry_scr[...]
    for p0 in reversed(range(0, pages, chunk_pages)):
        x = jnp.concatenate([lf_refs[p0 + i][...] for i in range(chunk_pages)], axis=0)
        hi, mid, lo = _split3(x)
        suffix = _dot(later, hi) + _dot(later, mid) + _dot(later, lo) + carry
        out_ref[p0 * page:p0 * page + CUMSUM_CHUNK, :] = suffix
        carry = suffix[0:1, :] + x[0:1, :]
    carry_scr[...] = carry


def _decode_bias(page_table, cache_logf, lf_new, *, pages=16):
    n, n_pages = page_table.shape
    _, page, h = cache_logf.shape
    groups = n_pages // pages
    kern = functools.partial(_decode_bias_kernel, pages=pages)

    def lf_spec(g):
        return pl.BlockSpec((None, page, h),
                            lambda i, s, pt: (pt[i, (groups - 1 - s) * pages + g], 0, 0))

    grid_spec = pltpu.PrefetchScalarGridSpec(
        num_scalar_prefetch=1,
        grid=(n, groups),
        in_specs=[pl.BlockSpec((None, 1, h), lambda i, s, pt: (i, 0, 0))]
                 + [lf_spec(g) for g in range(pages)],
        out_specs=pl.BlockSpec((None, pages * page, h), lambda i, s, pt: (i, groups - 1 - s, 0)),
        scratch_shapes=[pltpu.VMEM((1, h), _f32)],
    )
    return pl.pallas_call(
        kern, grid_spec=grid_spec,
        out_shape=jax.ShapeDtypeStruct((n, n_pages * page, h), _f32),
        compiler_params=_params("arbitrary", "arbitrary"), name="decode_bias",
    )(page_table, lf_new, *([cache_logf] * pages))


def _attn_decode_kernel(pt_ref, q_ref, bias_ref, kn_ref, vn_ref, *refs, pages, head_dim):
    k_refs, v_refs = refs[:pages], refs[pages:2 * pages]
    o_ref, m_scr, l_scr, acc_scr = refs[2 * pages:]
    n_heads, d = acc_scr.shape
    s_id = pl.program_id(1)

    @pl.when(s_id == 0)
    def _():
        m_scr[...] = jnp.full(m_scr.shape, NEG_BIG, _f32)
        l_scr[...] = jnp.zeros(l_scr.shape, _f32)
        acc_scr[...] = jnp.zeros(acc_scr.shape, _f32)

    head_of_lane = lax.broadcasted_iota(jnp.int32, (n_heads, d), 1) // head_dim
    own = head_of_lane == lax.broadcasted_iota(jnp.int32, (n_heads, d), 0)
    q_rows = jnp.broadcast_to(q_ref[...].astype(_f32), (n_heads, d))
    qbd = jnp.where(own, q_rows, 0.0).astype(_bf16)

    s = jnp.concatenate([_dot_nt(qbd, k_refs[g][...].astype(_bf16)) for g in range(pages)], axis=1)
    s = s + bias_ref[...]
    m_old = m_scr[...]
    m_new = jnp.maximum(m_old, jnp.max(s, axis=-1, keepdims=True))
    alpha = jnp.exp(m_old - m_new)
    p = jnp.exp(s - m_new)
    l_new = alpha * l_scr[...] + jnp.sum(p, axis=-1, keepdims=True)
    page = k_refs[0].shape[0]
    acc = alpha * acc_scr[...]
    for g in range(pages):
        acc = acc + _dot(p[:, g * page:(g + 1) * page].astype(_bf16), v_refs[g][...].astype(_bf16))
    m_scr[...] = m_new
    l_scr[...] = l_new
    acc_scr[...] = acc

    @pl.when(s_id == pl.num_programs(1) - 1)
    def _():
        s_self = jnp.sum(qbd.astype(_f32) * kn_ref[...], axis=-1, keepdims=True)
        m_fin = jnp.maximum(m_new, s_self)
        a_fin = jnp.exp(m_new - m_fin)
        p_self = jnp.exp(s_self - m_fin)
        l_fin = a_fin * l_new + p_self
        o = (a_fin * acc + p_self * vn_ref[...]) / l_fin
        o_ref[...] = jnp.sum(jnp.where(own, o, 0.0), axis=0, keepdims=True).astype(o_ref.dtype)


def _attn_decode(page_table, q, bias_t, k_new, v_new, cache_k, cache_v, *, head_dim, pages=8):
    n, n_pages = page_table.shape
    _, page, d = cache_k.shape
    n_heads = d // head_dim
    groups = n_pages // pages
    kern = functools.partial(_attn_decode_kernel, pages=pages, head_dim=head_dim)
    tok = pl.BlockSpec((None, 1, d), lambda i, s, pt: (i, 0, 0))

    def page_spec(g):
        return pl.BlockSpec((None, page, d), lambda i, s, pt: (pt[i, s * pages + g], 0, 0))

    grid_spec = pltpu.PrefetchScalarGridSpec(
        num_scalar_prefetch=1,
        grid=(n, groups),
        in_specs=[tok, pl.BlockSpec((None, n_heads, pages * page), lambda i, s, pt: (i, 0, s)),
                  tok, tok] + [page_spec(g) for g in range(pages)] + [page_spec(g) for g in range(pages)],
        out_specs=tok,
        scratch_shapes=[pltpu.VMEM((n_heads, 1), _f32), pltpu.VMEM((n_heads, 1), _f32),
                        pltpu.VMEM((n_heads, d), _f32)],
    )
    return pl.pallas_call(
        kern, grid_spec=grid_spec,
        out_shape=jax.ShapeDtypeStruct((n, 1, d), _bf16),
        compiler_params=_params("arbitrary", "arbitrary"), name="fox_attn_decode",
    )(page_table, q, bias_t, k_new, v_new, *([cache_k] * pages), *([cache_v] * pages))


def kernel(x_prompt, x_sample, cache_k, cache_v, cache_logf, state_conv, page_table, norm_mix, norm_ffn,
           w_pw1, w_dw, b_dw, conv_ln_g, conv_ln_b, w_pw2, norm_kv, w_k, w_v, w_f, b_f, w_q, w_o,
           w_gate, w_up, w_down, norm_final):
    b, t, d = x_prompt.shape
    n = x_sample.shape[0]
    n_heads = w_f.shape[1]
    head_dim = d // n_heads
    scale = head_dim ** -0.5
    n_pool, page = cache_k.shape[:2]
    assert w_pw1.shape[0] == 1 and w_q.shape[0] == 1 and x_sample.shape[1] == 1

    row = lambda a: a.reshape(1, -1)
    bf = lambda a: a.astype(_bf16)
    pw1, pw2 = bf(w_pw1[0]), bf(w_pw2[0])
    wk, wv, wq, wo = bf(w_k), bf(w_v), bf(w_q[0]), bf(w_o[0])
    wf_pad = bf(jnp.pad(w_f, ((0, 0), (0, LANES - n_heads))))
    wg, wu, wd = bf(w_gate), bf(w_up), bf(w_down)
    conv_w = (row(norm_mix[0]), pw1, w_dw[0], row(b_dw[0]), row(conv_ln_g[0]), row(conv_ln_b[0]), pw2)
    kvq_w = (row(norm_kv), row(norm_mix[1]), wk, wv, wf_pad, row(b_f), wq)

    hp, conv_state_p = _conv_prompt(x_prompt, *conv_w)
    hp = _ffn(hp.reshape(b * t, d), row(norm_ffn[0]), wg[0], wu[0], wd[0], tm=512)
    kp, vp, lfp, kp_b, vp_b, qp_b = _kvq(hp, *kvq_w, tm=512, scale=scale)
    lf_t = lfp.reshape(b, t, n_heads).transpose(0, 2, 1)
    f_p = _cumsum_time(lf_t).reshape(b, n_heads // 2, 2, t)
    op = _attn_prompt(qp_b.reshape(b, t, d), kp_b.reshape(b, t, d), vp_b.reshape(b, t, d), f_p,
                      head_dim=head_dim)
    y_prompt = _ffn(hp, row(norm_ffn[1]), wg[1], wu[1], wd[1], tm=512,
                    o=op.reshape(b * t, d), w_o=wo, g_final=row(norm_final))

    state_t = state_conv[0].transpose(1, 0, 2)
    hs, u_s = _conv_sample(x_sample.reshape(n, d), state_t, *conv_w)
    hs = _ffn(hs, row(norm_ffn[0]), wg[0], wu[0], wd[0], tm=n)
    ks, vs, lfs, _, _, qs_b = _kvq(hs, *kvq_w, tm=n, scale=scale)
    bias = _decode_bias(page_table, cache_logf, lfs.reshape(n, 1, n_heads))
    os_ = _attn_decode(page_table, qs_b.reshape(n, 1, d), bias.transpose(0, 2, 1),
                       ks.reshape(n, 1, d), vs.reshape(n, 1, d),
                       cache_k.reshape(n_pool, page, d), cache_v.reshape(n_pool, page, d),
                       head_dim=head_dim)
    y_sample = _ffn(hs, row(norm_ffn[1]), wg[1], wu[1], wd[1], tm=n,
                    o=os_.reshape(n, d), w_o=wo, g_final=row(norm_final))

    conv_state_s = jnp.concatenate([state_conv[0][:, 1:], u_s[:, None, :]], axis=1)
    heads = lambda a, m: a.reshape(m, -1, n_heads, head_dim)
    return (y_prompt.reshape(b, t, d), y_sample.reshape(n, 1, d),
            conv_state_p[None], conv_state_s[None],
            heads(kp, b), heads(vp, b), lfp.reshape(b, t, n_heads),
            heads(ks, n), heads(vs, n), lfs.reshape(n, 1, n_heads))
```
